```python
import jax, jax.numpy as jnp
from jax import lax
import numpy as np


D_MODEL = 1024
BATCH = 4
SEQ = 4096
DEPTH = 4

D_MIX = D_MODEL
D_CONV = D_MIX // 4
D_CONF = D_MIX // 4
D_DN = D_MIX // 2
N_CONV_GROUPS = 4
N_CONF_GROUPS = 4
DN_HEADS = 4
DN_HEAD_DIM = D_DN // DN_HEADS
SHORT_CONV_W = 3
CONF_CONV_W = 31
DN_CONV_W = 4
DN_CHUNK = 64
D_FF = ((8 * D_MODEL + 3 * 256 - 1) // (3 * 256)) * 256
IN_COLS = 3 * D_CONV + 2 * D_CONF + 4 * D_DN + 2 * DN_HEADS
N_MOD = 6
EPS = 1e-6

kernel_name = 'hymba_conv_conformer_gdn_adaln_trunk'


def rmsnorm(x, g):
    xf = x.astype(jnp.float32)
    y = xf * lax.rsqrt(jnp.mean(xf * xf, axis=-1, keepdims=True) + EPS)
    return y.astype(x.dtype) * g


def layernorm(x, g, b):
    xf = x.astype(jnp.float32)
    mu = jnp.mean(xf, axis=-1, keepdims=True)
    var = jnp.mean(jnp.square(xf - mu), axis=-1, keepdims=True)
    return ((xf - mu) * lax.rsqrt(var + 1e-5)).astype(x.dtype) * g + b


def causal_dwconv(x, w):
    K, C = w.shape
    xp = jnp.pad(x, ((0, 0), (K - 1, 0), (0, 0)))
    return lax.conv_general_dilated(xp, w[:, None, :].astype(x.dtype), window_strides=(1,), padding='VALID',
                                    dimension_numbers=('NWC', 'WIO', 'NWC'), feature_group_count=C)


def l2norm(x):
    return x * lax.rsqrt(jnp.sum(x * x, axis=-1, keepdims=True) + EPS)


def chunk_gated_delta_rule(q, k, v, g, beta):
    Bsz, T, H, Dk = q.shape
    C = DN_CHUNK
    N = T // C
    def to_chunks(t):
        return t.reshape(Bsz, N, C, H, -1).transpose(0, 3, 1, 2, 4)
    q = to_chunks(q) * (Dk ** -0.5)
    k = to_chunks(k)
    v = to_chunks(v)
    beta = beta.reshape(Bsz, N, C, H).transpose(0, 3, 1, 2)
    g = jnp.cumsum(g.reshape(Bsz, N, C, H).transpose(0, 3, 1, 2), axis=-1)
    causal = jnp.tril(jnp.ones((C, C), dtype=bool))
    strict = jnp.tril(jnp.ones((C, C), dtype=bool), -1)
    diff = g[..., :, None] - g[..., None, :]
    decay = jnp.where(causal, jnp.exp(jnp.where(causal, diff, 0.0)), 0.0)
    k_beta = k * beta[..., None]
    v_beta = v * beta[..., None]
    Lm = jnp.where(strict, jnp.einsum('bhncd,bhnsd->bhncs', k_beta, k) * decay, 0.0)
    eye = jnp.eye(C, dtype=q.dtype)
    Tm = lax.linalg.triangular_solve(eye + Lm, jnp.broadcast_to(eye, Lm.shape), left_side=True,
                                     lower=True, unit_diagonal=True)
    u = jnp.einsum('bhncs,bhnse->bhnce', Tm, v_beta)
    w = jnp.einsum('bhncs,bhnsd->bhncd', Tm, k_beta * jnp.exp(g)[..., None])
    qk = jnp.where(causal, jnp.einsum('bhncd,bhnsd->bhncs', q, k) * decay, 0.0)

    def step(S, inp):
        q_i, k_i, u_i, w_i, g_i, qk_i = inp
        v_new = u_i - jnp.einsum('bhcd,bhde->bhce', w_i, S)
        o = (jnp.einsum('bhcd,bhde->bhce', q_i * jnp.exp(g_i)[..., None], S)
             + jnp.einsum('bhcs,bhse->bhce', qk_i, v_new))
        g_last = g_i[..., -1]
        S = (S * jnp.exp(g_last)[..., None, None]
             + jnp.einsum('bhcd,bhce->bhde', k_i * jnp.exp(g_last[..., None] - g_i)[..., None], v_new))
        return S, o

    xs = tuple(jnp.moveaxis(t, 2, 0) for t in (q, k, u, w, g, qk))
    S0 = jnp.zeros((Bsz, H, Dk, v.shape[-1]), jnp.float32)
    _, o = lax.scan(step, S0, xs)
    return o.transpose(1, 0, 3, 2, 4).reshape(Bsz, T, H, -1)


def setup_inputs(seed: int = 0) -> dict:
    key = jax.random.key(seed)
    ks = jax.random.split(key, 24)
    f32 = jnp.float32
    def nrm(k, shape, scale):
        return jax.random.normal(k, shape, f32) * scale
    L, D = DEPTH, D_MODEL
    dt = jax.random.uniform(ks[13], (L, DN_HEADS), f32, minval=1e-3, maxval=0.1)
    return {
        'x': nrm(ks[0], (BATCH, SEQ, D), 1.0),
        'c': nrm(ks[1], (BATCH, D), 1.0),
        'w_ada': nrm(ks[2], (L, D, N_MOD * D), 0.5 * D ** -0.5),
        'b_ada': nrm(ks[3], (L, N_MOD * D), 0.02),
        'norm_mix_g': 1.0 + nrm(ks[4], (L, D), 0.02),
        'norm_ffn_g': 1.0 + nrm(ks[5], (L, D), 0.02),
        'w_in': nrm(ks[6], (L, D, IN_COLS), D ** -0.5),
        'conv_a_w': nrm(ks[7], (L, SHORT_CONV_W, D_CONV), SHORT_CONV_W ** -0.5),
        'conf_dw_w': nrm(ks[8], (L, CONF_CONV_W, D_CONF), CONF_CONV_W ** -0.5),
        'conf_dw_b': nrm(ks[9], (L, D_CONF), 0.02),
        'conf_ln_g': 1.0 + nrm(ks[10], (L, D_CONF), 0.02),
        'conf_ln_b': nrm(ks[11], (L, D_CONF), 0.02),
        'dn_conv_w': nrm(ks[12], (L, DN_CONV_W, 3 * D_DN), DN_CONV_W ** -0.5),
        'dn_a_log': jnp.log(jax.random.uniform(ks[14], (L, DN_HEADS), f32, minval=1.0, maxval=16.0)),
        'dn_dt_bias': dt + jnp.log(-jnp.expm1(-dt)),
        'dn_norm_g': 1.0 + nrm(ks[15], (L, DN_HEAD_DIM), 0.02),
        'w_out': nrm(ks[16], (L, D_MIX, D), D_MIX ** -0.5),
        'w_ffn_in': nrm(ks[17], (L, D, 2 * D_FF), D ** -0.5),
        'w_ffn_out': nrm(ks[18], (L, D_FF, D), D_FF ** -0.5),
        'final_norm_g': 1.0 + nrm(ks[19], (D,), 0.02),
    }


def reference(x, c, w_ada, b_ada, norm_mix_g, norm_ffn_g, w_in, conv_a_w, conf_dw_w, conf_dw_b,
              conf_ln_g, conf_ln_b, dn_conv_w, dn_a_log, dn_dt_bias, dn_norm_g, w_out,
              w_ffn_in, w_ffn_out, final_norm_g):
    Bsz, T, _ = x.shape
    c_act = jax.nn.silu(c)
    sizes = [D_CONV] * 3 + [D_CONF] * 2 + [D_DN] * 4 + [DN_HEADS] * 2
    split_idx = np.cumsum(sizes)[:-1].tolist()
    for l in range(DEPTH):
        mod = (c_act @ w_ada[l] + b_ada[l])[:, None, :]
        sh1, sc1, g1, sh2, sc2, g2 = jnp.split(mod, N_MOD, axis=-1)

        h = rmsnorm(x, norm_mix_g[l]) * (1.0 + sc1) + sh1
        proj = h @ w_in[l]
        (a_b, a_c, a_v, b_a, b_g, c_q, c_k, c_v, c_z, c_alpha, c_beta) = jnp.split(proj, split_idx, axis=-1)

        y_a = a_b * causal_dwconv(a_c * a_v, conv_a_w[l])

        u = b_a * jax.nn.sigmoid(b_g)
        u = causal_dwconv(u, conf_dw_w[l]) + conf_dw_b[l]
        y_b = jax.nn.silu(layernorm(u, conf_ln_g[l], conf_ln_b[l]))

        qkv = jax.nn.silu(causal_dwconv(jnp.concatenate([c_q, c_k, c_v], axis=-1), dn_conv_w[l]))
        q, k, v = jnp.split(qkv.astype(jnp.float32), 3, axis=-1)
        q = l2norm(q.reshape(Bsz, T, DN_HEADS, DN_HEAD_DIM))
        k = l2norm(k.reshape(Bsz, T, DN_HEADS, DN_HEAD_DIM))
        v = v.reshape(Bsz, T, DN_HEADS, DN_HEAD_DIM)
        gdec = -jnp.exp(dn_a_log[l].astype(jnp.float32)) * jax.nn.softplus(
            c_alpha.astype(jnp.float32) + dn_dt_bias[l].astype(jnp.float32))
        beta = jax.nn.sigmoid(c_beta.astype(jnp.float32))
        o = chunk_gated_delta_rule(q, k, v, gdec, beta).astype(x.dtype)
        z = c_z.reshape(Bsz, T, DN_HEADS, DN_HEAD_DIM)
        y_c = (rmsnorm(o, dn_norm_g[l]) * jax.nn.silu(z)).reshape(Bsz, T, D_DN)

        mix = jnp.concatenate([y_a, y_b, y_c], axis=-1) @ w_out[l]
        x = x + g1 * mix

        h = rmsnorm(x, norm_ffn_g[l]) * (1.0 + sc2) + sh2
        gate, up = jnp.split(h @ w_ffn_in[l], 2, axis=-1)
        x = x + g2 * ((jax.nn.silu(gate) * up) @ w_ffn_out[l])
    return rmsnorm(x, final_norm_g)
```

```python
import functools

import jax
import jax.numpy as jnp
from jax import lax
from jax.experimental import pallas as pl
from jax.experimental.pallas import tpu as pltpu

F32 = jnp.float32
BF16 = jnp.bfloat16

D_CONV = 256
D_CONF = 256
D_DN = 512
DN_HEADS = 4
DN_HEAD_DIM = 128
SHORT_CONV_W = 3
CONF_CONV_W = 31
DN_CONV_W = 4
CHUNK = 64
N_MOD = 6
EPS = 1e-6
LN_EPS = 1e-5
Q_SCALE = DN_HEAD_DIM ** -0.5

LANE = 128
SUBLANE = 8
VMEM_LIMIT = 56 * 1024 * 1024

COL_A_B = 0
COL_A_C = COL_A_B + D_CONV
COL_A_V = COL_A_C + D_CONV
COL_B_A = COL_A_V + D_CONV
COL_B_G = COL_B_A + D_CONF
COL_QKV = COL_B_G + D_CONF
COL_Z = COL_QKV + 3 * D_DN
COL_AB = COL_Z + D_DN
IN_COLS = COL_AB + 2 * DN_HEADS
IN_COLS_PAD = COL_AB + LANE

HIST_A = SUBLANE
HIST_B = 32
HIST_C = SUBLANE


def _dot(a, b):
    return jnp.dot(a, b, preferred_element_type=F32)


def _dot_nt(a, b):
    return lax.dot_general(a, b, (((1,), (1,)), ((), ())), preferred_element_type=F32)


def _dot_tn(a, b):
    return lax.dot_general(a, b, (((0,), (0,)), ((), ())), preferred_element_type=F32)


def _split3(a):
    hi = a.astype(BF16)
    r1 = a - hi.astype(F32)
    mid = r1.astype(BF16)
    lo = (r1 - mid.astype(F32)).astype(BF16)
    return hi, mid, lo


def _silu(v):
    return v * jax.nn.sigmoid(v)


def _softplus(v):
    return jnp.maximum(v, 0.0) + jnp.log1p(jnp.exp(-jnp.abs(v)))


def _ada_kernel(c_ref, w_ref, b_ref, o_ref):
    ca = _silu(c_ref[...])
    o_ref[...] = _dot(ca.astype(BF16), w_ref[...].astype(BF16)) + b_ref[...]


def _ada_call(c_pad, w_ada, b_ada):
    depth, d, n = w_ada.shape
    rows = c_pad.shape[0]
    tn = 1536
    return pl.pallas_call(
        _ada_kernel,
        grid=(depth, n // tn),
        in_specs=[
            pl.BlockSpec((rows, d), lambda l, j: (0, 0)),
            pl.BlockSpec((None, d, tn), lambda l, j: (l, 0, j)),
            pl.BlockSpec((None, 1, tn), lambda l, j: (l, 0, j)),
        ],
        out_specs=pl.BlockSpec((None, rows, tn), lambda l, j: (l, 0, j)),
        out_shape=jax.ShapeDtypeStruct((depth, rows, n), F32),
        compiler_params=pltpu.CompilerParams(
            dimension_semantics=("arbitrary", "arbitrary"), vmem_limit_bytes=VMEM_LIMIT),
        name="adaln_mod",
    )(c_pad, w_ada, b_ada.reshape(depth, 1, n))


def _modulated_rmsnorm(x, g, scale, shift):
    ms = jnp.mean(x * x, axis=-1, keepdims=True)
    return x * lax.rsqrt(ms + EPS) * (g * (1.0 + scale)) + shift


def _inproj_kernel(x_ref, mod_ref, g_ref, w_ref, o_ref):
    h = _modulated_rmsnorm(x_ref[...], g_ref[...], mod_ref[1:2, :], mod_ref[0:1, :])
    o_ref[...] = _dot(h.astype(BF16), w_ref[...])


def _inproj_call(x, mod, g, w, tm):
    b, t, d = x.shape
    n = w.shape[1]
    return pl.pallas_call(
        _inproj_kernel,
        grid=(b, t // tm),
        in_specs=[
            pl.BlockSpec((None, tm, d), lambda i, j: (i, j, 0)),
            pl.BlockSpec((None, SUBLANE, d), lambda i, j: (i, 0, 0)),
            pl.BlockSpec((1, d), lambda i, j: (0, 0)),
            pl.BlockSpec((d, n), lambda i, j: (0, 0)),
        ],
        out_specs=pl.BlockSpec((None, tm, n), lambda i, j: (i, j, 0)),
        out_shape=jax.ShapeDtypeStruct((b, t, n), F32),
        compiler_params=pltpu.CompilerParams(
            dimension_semantics=("arbitrary", "arbitrary"), vmem_limit_bytes=VMEM_LIMIT),
        name="inproj",
    )(x, mod, g, w)


def _neumann_unit_lower_inverse_minus_eye(a):
    x = -a
    y = x
    p = _dot(x.astype(BF16), x.astype(BF16))
    m = 2
    while m < CHUNK:
        pb = p.astype(BF16)
        if 2 * m < CHUNK:
            prod = _dot(pb, jnp.concatenate([y, p], axis=1).astype(BF16))
            y = y + p + prod[:, :CHUNK]
            p = prod[:, CHUNK:]
        else:
            y = y + p + _dot(pb, y.astype(BF16))
        m *= 2
    return y


def _delta_chunk_head(qh, kh, vh, bb, gcb, r_row, s, causal, strict):
    eg = jnp.exp(gcb)
    kb = kh * bb
    vb = vh * bb
    kbg = kb * eg
    qs = qh * Q_SCALE
    kbf = kh.astype(BF16)
    diff = gcb[:, :CHUNK] - r_row
    decay = jnp.where(causal, jnp.exp(jnp.where(causal, diff, 0.0)), 0.0)
    a = jnp.where(strict, _dot_nt(kb.astype(BF16), kbf) * decay, 0.0)
    qk = _dot_nt(qs.astype(BF16), kbf) * decay
    y = _neumann_unit_lower_inverse_minus_eye(a).astype(BF16)
    uw = _dot(y, jnp.concatenate([vb, kbg], axis=1).astype(BF16))
    u = vb + uw[:, :DN_HEAD_DIM]
    w = kbg + uw[:, DN_HEAD_DIM:]
    sb = s.astype(BF16)
    v_new = u - _dot(w.astype(BF16), sb)
    v_new_b = v_new.astype(BF16)
    o = _dot((qs * eg).astype(BF16), sb) + _dot(qk.astype(BF16), v_new_b)
    g_last = gcb[CHUNK - 1:CHUNK, :]
    kt = kh * jnp.exp(g_last - gcb)
    s_new = s * jnp.exp(g_last) + _dot_tn(kt.astype(BF16), v_new_b)
    return o, s_new


def _mix_kernel(proj_ref, x_ref, mod_ref, cwa_ref, cwb_ref, cbb_ref, lng_ref, lnb_ref, cwc_ref,
                alog_ref, dtb_ref, dng_ref, wout_ref, o_ref,
                ha_ref, hb_ref, hc_ref, s_ref, y_ref):
    tm = x_ref.shape[0]
    n_chunks = tm // CHUNK

    @pl.when(pl.program_id(1) == 0)
    def _():
        ha_ref[0:HIST_A, :] = jnp.zeros((HIST_A, D_CONV), F32)
        hb_ref[0:HIST_B, :] = jnp.zeros((HIST_B, D_CONF), F32)
        hc_ref[0:HIST_C, :] = jnp.zeros((HIST_C, 3 * D_DN), F32)
        s_ref[...] = jnp.zeros(s_ref.shape, F32)

    ha_ref[HIST_A:HIST_A + tm, :] = proj_ref[:, COL_A_C:COL_A_V] * proj_ref[:, COL_A_V:COL_B_A]
    hb_ref[HIST_B:HIST_B + tm, :] = (
        proj_ref[:, COL_B_A:COL_B_G] * jax.nn.sigmoid(proj_ref[:, COL_B_G:COL_QKV]))
    hc_ref[HIST_C:HIST_C + tm, :] = proj_ref[:, COL_QKV:COL_Z]

    ab = proj_ref[:, COL_AB:COL_AB + LANE]
    g_full = -jnp.exp(alog_ref[...]) * _softplus(ab + dtb_ref[...])
    beta_full = jax.nn.sigmoid(ab)
    r_i = lax.broadcasted_iota(jnp.int32, (tm, tm), 0)
    c_i = lax.broadcasted_iota(jnp.int32, (tm, tm), 1)
    same_chunk = (r_i // CHUNK) == (c_i // CHUNK)
    bd_tril = jnp.where(same_chunk, jnp.where(r_i >= c_i, 1.0, 0.0), 0.0).astype(BF16)
    g_hi, g_mid, g_lo = _split3(g_full)
    gc_full = _dot(bd_tril, g_hi) + _dot(bd_tril, g_mid) + _dot(bd_tril, g_lo)
    e_r = lax.broadcasted_iota(jnp.int32, (LANE, D_DN), 0)
    e_h = lax.broadcasted_iota(jnp.int32, (LANE, D_DN), 1) // DN_HEAD_DIM
    sel_g = jnp.where(e_r == e_h, 1.0, 0.0).astype(BF16)
    sel_b = jnp.where(e_r == e_h + DN_HEADS, 1.0, 0.0).astype(BF16)
    c_hi, c_mid, c_lo = _split3(gc_full)
    gcb_all = _dot(c_hi, sel_g) + _dot(c_mid, sel_g) + _dot(c_lo, sel_g)
    b_hi, b_mid, b_lo = _split3(beta_full)
    bb_all = _dot(b_hi, sel_b) + _dot(b_mid, sel_b) + _dot(b_lo, sel_b)
    eye = jnp.where(lax.broadcasted_iota(jnp.int32, (LANE, LANE), 0)
                    == lax.broadcasted_iota(jnp.int32, (LANE, LANE), 1), 1.0, 0.0).astype(BF16)
    g_t = _dot_nt(eye, c_hi) + _dot_nt(eye, c_mid) + _dot_nt(eye, c_lo)

    row = lax.broadcasted_iota(jnp.int32, (CHUNK, CHUNK), 0)
    col = lax.broadcasted_iota(jnp.int32, (CHUNK, CHUNK), 1)
    causal = row >= col
    strict = row > col

    for n in range(n_chunks):
        r0 = n * CHUNK
        conv_a = cwa_ref[0:1, :] * ha_ref[HIST_A - 2 + r0:HIST_A - 2 + r0 + CHUNK, :]
        for k in range(1, SHORT_CONV_W):
            o0 = HIST_A - (SHORT_CONV_W - 1) + k + r0
            conv_a = conv_a + cwa_ref[k:k + 1, :] * ha_ref[o0:o0 + CHUNK, :]
        y_ref[r0:r0 + CHUNK, 0:D_CONV] = proj_ref[r0:r0 + CHUNK, COL_A_B:COL_A_C] * conv_a

        acc = cbb_ref[...] + cwb_ref[0:1, :] * hb_ref[HIST_B - (CONF_CONV_W - 1) + r0:
                                                      HIST_B - (CONF_CONV_W - 1) + r0 + CHUNK, :]
        for k in range(1, CONF_CONV_W):
            o0 = HIST_B - (CONF_CONV_W - 1) + k + r0
            acc = acc + cwb_ref[k:k + 1, :] * hb_ref[o0:o0 + CHUNK, :]
        mu = jnp.mean(acc, axis=-1, keepdims=True)
        dev = acc - mu
        var = jnp.mean(dev * dev, axis=-1, keepdims=True)
        yb = dev * lax.rsqrt(var + LN_EPS) * lng_ref[...] + lnb_ref[...]
        y_ref[r0:r0 + CHUNK, D_CONV:D_CONV + D_CONF] = _silu(yb)

        o0 = HIST_C - (DN_CONV_W - 1) + r0
        qkv = cwc_ref[0:1, :] * hc_ref[o0:o0 + CHUNK, :]
        for k in range(1, DN_CONV_W):
            qkv = qkv + cwc_ref[k:k + 1, :] * hc_ref[o0 + k:o0 + k + CHUNK, :]
        qkv = _silu(qkv)
        for h in range(DN_HEADS):
            l0 = h * DN_HEAD_DIM
            qh = qkv[:, l0:l0 + DN_HEAD_DIM]
            kh = qkv[:, D_DN + l0:D_DN + l0 + DN_HEAD_DIM]
            vh = qkv[:, 2 * D_DN + l0:2 * D_DN + l0 + DN_HEAD_DIM]
            qh = qh * lax.rsqrt(jnp.sum(qh * qh, axis=-1, keepdims=True) + EPS)
            kh = kh * lax.rsqrt(jnp.sum(kh * kh, axis=-1, keepdims=True) + EPS)
            o, s_new = _delta_chunk_head(
                qh, kh, vh,
                bb_all[r0:r0 + CHUNK, l0:l0 + DN_HEAD_DIM],
                gcb_all[r0:r0 + CHUNK, l0:l0 + DN_HEAD_DIM],
                g_t[h:h + 1, r0:r0 + CHUNK],
                s_ref[h], causal, strict)
            s_ref[h] = s_new
            z = proj_ref[r0:r0 + CHUNK, COL_Z + l0:COL_Z + l0 + DN_HEAD_DIM]
            on = o * lax.rsqrt(jnp.mean(o * o, axis=-1, keepdims=True) + EPS) * dng_ref[...]
            c0 = D_CONV + D_CONF + l0
            y_ref[r0:r0 + CHUNK, c0:c0 + DN_HEAD_DIM] = on * _silu(z)

    ha_ref[0:HIST_A, :] = ha_ref[tm:tm + HIST_A, :]
    hb_ref[0:HIST_B, :] = hb_ref[tm:tm + HIST_B, :]
    hc_ref[0:HIST_C, :] = hc_ref[tm:tm + HIST_C, :]

    mix = _dot(y_ref[...].astype(BF16), wout_ref[...])
    o_ref[...] = x_ref[...] + mod_ref[2:3, :] * mix


def _full(shape):
    nd = len(shape)
    return pl.BlockSpec(shape, lambda i, j: (0,) * nd)


def _mix_call(proj, x, mod, cwa, cwb, cbb, lng, lnb, cwc, alog, dtb, dng, wout, tm):
    b, t, d = x.shape
    n = proj.shape[2]
    small = [cwa, cwb, cbb, lng, lnb, cwc, alog, dtb, dng, wout]
    return pl.pallas_call(
        _mix_kernel,
        grid=(b, t // tm),
        in_specs=[
            pl.BlockSpec((None, tm, n), lambda i, j: (i, j, 0)),
            pl.BlockSpec((None, tm, d), lambda i, j: (i, j, 0)),
            pl.BlockSpec((None, SUBLANE, d), lambda i, j: (i, 0, 0)),
        ] + [_full(a.shape) for a in small],
        out_specs=pl.BlockSpec((None, tm, d), lambda i, j: (i, j, 0)),
        out_shape=jax.ShapeDtypeStruct((b, t, d), F32),
        scratch_shapes=[
            pltpu.VMEM((HIST_A + tm, D_CONV), F32),
            pltpu.VMEM((HIST_B + tm, D_CONF), F32),
            pltpu.VMEM((HIST_C + tm, 3 * D_DN), F32),
            pltpu.VMEM((DN_HEADS, DN_HEAD_DIM, DN_HEAD_DIM), F32),
            pltpu.VMEM((tm, d), F32),
        ],
        compiler_params=pltpu.CompilerParams(
            dimension_semantics=("arbitrary", "arbitrary"), vmem_limit_bytes=VMEM_LIMIT),
        name="mix",
    )(proj, x, mod, *small)


def _ffn_kernel(x_ref, mod_ref, g_ref, w1_ref, w2_ref, fg_ref, o_ref, a_ref, *, final, n_split):
    x = x_ref[...]
    h = _modulated_rmsnorm(x, g_ref[...], mod_ref[4:5, :], mod_ref[3:4, :]).astype(BF16)
    d_ff = w2_ref.shape[0]
    step = d_ff // n_split
    for j in range(n_split):
        c0 = j * step
        gate = _dot(h, w1_ref[:, c0:c0 + step])
        up = _dot(h, w1_ref[:, d_ff + c0:d_ff + c0 + step])
        a_ref[:, c0:c0 + step] = (_silu(gate) * up).astype(BF16)
    xn = x + mod_ref[5:6, :] * _dot(a_ref[...], w2_ref[...])
    if final:
        xn = xn * lax.rsqrt(jnp.mean(xn * xn, axis=-1, keepdims=True) + EPS) * fg_ref[...]
    o_ref[...] = xn


def _ffn_call(x, mod, g, w1, w2, fg, tm, final):
    b, t, d = x.shape
    d_ff = w2.shape[0]
    return pl.pallas_call(
        functools.partial(_ffn_kernel, final=final, n_split=2),
        grid=(b, t // tm),
        in_specs=[
            pl.BlockSpec((None, tm, d), lambda i, j: (i, j, 0)),
            pl.BlockSpec((None, SUBLANE, d), lambda i, j: (i, 0, 0)),
            _full(g.shape), _full(w1.shape), _full(w2.shape), _full(fg.shape),
        ],
        out_specs=pl.BlockSpec((None, tm, d), lambda i, j: (i, j, 0)),
        out_shape=jax.ShapeDtypeStruct((b, t, d), F32),
        scratch_shapes=[pltpu.VMEM((tm, d_ff), BF16)],
        compiler_params=pltpu.CompilerParams(
            dimension_semantics=("arbitrary", "arbitrary"), vmem_limit_bytes=VMEM_LIMIT),
        name="ffn",
    )(x, mod, g, w1, w2, fg)


def _lane_row(v):
    return jnp.pad(v.astype(F32), (0, LANE - v.shape[0]))[None, :]


def kernel(x, c, w_ada, b_ada, norm_mix_g, norm_ffn_g, w_in, conv_a_w, conf_dw_w, conf_dw_b,
           conf_ln_g, conf_ln_b, dn_conv_w, dn_a_log, dn_dt_bias, dn_norm_g, w_out,
           w_ffn_in, w_ffn_out, final_norm_g):
    bsz, t, d = x.shape
    depth = w_ada.shape[0]
    assert t % CHUNK == 0 and bsz <= SUBLANE
    tm_mix = min(256, t)
    tm_mm = min(256, t)
    tm_ffn = min(512, t)

    c_pad = jnp.pad(c, ((0, SUBLANE - bsz), (0, 0)))
    mod_all = _ada_call(c_pad, w_ada, b_ada)
    mod_all = mod_all[:, :bsz].reshape(depth, bsz, N_MOD, d)
    mod_all = jnp.pad(mod_all, ((0, 0), (0, 0), (0, SUBLANE - N_MOD), (0, 0)))
    w_in_p = jnp.pad(w_in, ((0, 0), (0, 0), (0, IN_COLS_PAD - IN_COLS))).astype(BF16)
    w_out_b = w_out.astype(BF16)
    w1_b = w_ffn_in.astype(BF16)
    w2_b = w_ffn_out.astype(BF16)

    for l in range(depth):
        mod = mod_all[l]
        proj = _inproj_call(x, mod, norm_mix_g[l][None, :], w_in_p[l], tm_mm)
        x = _mix_call(proj, x, mod, conv_a_w[l], conf_dw_w[l], conf_dw_b[l][None, :],
                      conf_ln_g[l][None, :], conf_ln_b[l][None, :], dn_conv_w[l],
                      _lane_row(dn_a_log[l]), _lane_row(dn_dt_bias[l]), dn_norm_g[l][None, :],
                      w_out_b[l], tm_mix)
        x = _ffn_call(x, mod, norm_ffn_g[l][None, :], w1_b[l], w2_b[l], final_norm_g[None, :],
                      tm_ffn, final=(l == depth - 1))
    return x
```

```python
import functools

import jax
import jax.numpy as jnp
from jax import lax
from jax.experimental import pallas as pl
from jax.experimental.pallas import tpu as pltpu

F32 = jnp.float32
BF16 = jnp.bfloat16

D_CONV = 256
D_CONF = 256
D_DN = 512
DN_HEADS = 4
DN_HEAD_DIM = 128
SHORT_CONV_W = 3
CONF_CONV_W = 31
DN_CONV_W = 4
CHUNK = 64
N_MOD = 6
EPS = 1e-6
LN_EPS = 1e-5
Q_SCALE = DN_HEAD_DIM ** -0.5

LANE = 128
SUBLANE = 8
VMEM_LIMIT = 56 * 1024 * 1024

COL_A_B = 0
COL_A_C = COL_A_B + D_CONV
COL_A_V = COL_A_C + D_CONV
COL_B_A = COL_A_V + D_CONV
COL_B_G = COL_B_A + D_CONF
COL_QKV = COL_B_G + D_CONF
COL_Z = COL_QKV + 3 * D_DN
COL_AB = COL_Z + D_DN
IN_COLS = COL_AB + 2 * DN_HEADS
IN_COLS_PAD = COL_AB + LANE

HIST_A = SUBLANE
HIST_B = 32
HIST_C = SUBLANE


def _dot(a, b):
    return jnp.dot(a, b, preferred_element_type=F32)


def _dot_nt(a, b):
    return lax.dot_general(a, b, (((1,), (1,)), ((), ())), preferred_element_type=F32)


def _dot_tn(a, b):
    return lax.dot_general(a, b, (((0,), (0,)), ((), ())), preferred_element_type=F32)


def _split3(a):
    hi = a.astype(BF16)
    r1 = a - hi.astype(F32)
    mid = r1.astype(BF16)
    lo = (r1 - mid.astype(F32)).astype(BF16)
    return hi, mid, lo


def _silu(v):
    return v * jax.nn.sigmoid(v)


def _softplus(v):
    return jnp.maximum(v, 0.0) + jnp.log1p(jnp.exp(-jnp.abs(v)))


def _ada_kernel(c_ref, w_ref, b_ref, o_ref):
    ca = _silu(c_ref[...])
    o_ref[...] = _dot(ca.astype(BF16), w_ref[...].astype(BF16)) + b_ref[...]


def _ada_call(c_pad, w_ada, b_ada):
    depth, d, n = w_ada.shape
    rows = c_pad.shape[0]
    tn = 1536
    return pl.pallas_call(
        _ada_kernel,
        grid=(depth, n // tn),
        in_specs=[
            pl.BlockSpec((rows, d), lambda l, j: (0, 0)),
            pl.BlockSpec((None, d, tn), lambda l, j: (l, 0, j)),
            pl.BlockSpec((None, 1, tn), lambda l, j: (l, 0, j)),
        ],
        out_specs=pl.BlockSpec((None, rows, tn), lambda l, j: (l, 0, j)),
        out_shape=jax.ShapeDtypeStruct((depth, rows, n), F32),
        compiler_params=pltpu.CompilerParams(
            dimension_semantics=("arbitrary", "arbitrary"), vmem_limit_bytes=VMEM_LIMIT),
        name="adaln_mod",
    )(c_pad, w_ada, b_ada.reshape(depth, 1, n))


def _modulated_rmsnorm(x, g, scale, shift):
    ms = jnp.mean(x * x, axis=-1, keepdims=True)
    return x * lax.rsqrt(ms + EPS) * (g * (1.0 + scale)) + shift


def _inproj_kernel(x_ref, mod_ref, g_ref, w_ref, o_ref):
    h = _modulated_rmsnorm(x_ref[...], g_ref[...], mod_ref[1:2, :], mod_ref[0:1, :])
    o_ref[...] = _dot(h.astype(BF16), w_ref[...])


def _inproj_call(x, mod, g, w, tm):
    b, t, d = x.shape
    n = w.shape[1]
    return pl.pallas_call(
        _inproj_kernel,
        grid=(b, t // tm),
        in_specs=[
            pl.BlockSpec((None, tm, d), lambda i, j: (i, j, 0)),
            pl.BlockSpec((None, SUBLANE, d), lambda i, j: (i, 0, 0)),
            pl.BlockSpec((1, d), lambda i, j: (0, 0)),
            pl.BlockSpec((d, n), lambda i, j: (0, 0)),
        ],
        out_specs=pl.BlockSpec((None, tm, n), lambda i, j: (i, j, 0)),
        out_shape=jax.ShapeDtypeStruct((b, t, n), F32),
        compiler_params=pltpu.CompilerParams(
            dimension_semantics=("arbitrary", "arbitrary"), vmem_limit_bytes=VMEM_LIMIT),
        name="inproj",
    )(x, mod, g, w)


def _roll_down_slabs(slabs, s):
    c = slabs[0].shape[1]
    take_prev = lax.broadcasted_iota(jnp.int32, (SUBLANE, c), 0) < s
    rots = [pltpu.roll(v, s, axis=0) for v in slabs]
    return [jnp.where(take_prev, rots[i - 1], rots[i]) for i in range(1, len(slabs))]


def _short_conv_chunk(hist_ref, w_ref, r0, taps):
    n_slab = CHUNK // SUBLANE
    slabs = [hist_ref[r0 + SUBLANE * i:r0 + SUBLANE * (i + 1), :] for i in range(n_slab + 1)]
    acc = [w_ref[taps - 1] * v for v in slabs[1:]]
    for s in range(1, taps):
        delayed = _roll_down_slabs(slabs, s)
        acc = [a + w_ref[taps - 1 - s] * v for a, v in zip(acc, delayed)]
    return jnp.concatenate(acc, axis=0)


def _mix_kernel(proj_ref, x_ref, mod_ref, cwa_ref, cwb_ref, cbb_ref, lng_ref, lnb_ref, cwc_ref,
                alog_ref, dtb_ref, dng_ref, wout_ref, o_ref,
                ha_ref, hb_ref, hc_ref, s_ref, y_ref, qkv_ref, u_ref, w_ref, qe_ref, kt_ref,
                qk_ref, hbd_ref):
    tm = x_ref.shape[0]
    n_chunks = tm // CHUNK
    items = [(n, h) for n in range(n_chunks) for h in range(DN_HEADS)]

    @pl.when(pl.program_id(1) == 0)
    def _():
        ha_ref[0:HIST_A, :] = jnp.zeros((HIST_A, D_CONV), F32)
        hb_ref[0:HIST_B, :] = jnp.zeros((HIST_B, D_CONF), F32)
        hc_ref[0:HIST_C, :] = jnp.zeros((HIST_C, 3 * D_DN), F32)
        s_ref[...] = jnp.zeros(s_ref.shape, F32)

    ha_ref[HIST_A:HIST_A + tm, :] = proj_ref[:, COL_A_C:COL_A_V] * proj_ref[:, COL_A_V:COL_B_A]
    hb_ref[HIST_B:HIST_B + tm, :] = (
        proj_ref[:, COL_B_A:COL_B_G] * jax.nn.sigmoid(proj_ref[:, COL_B_G:COL_QKV]))
    hc_ref[HIST_C:HIST_C + tm, :] = proj_ref[:, COL_QKV:COL_Z]
    hb_slabs = [hb_ref[SUBLANE * i:SUBLANE * (i + 1), :] for i in range((HIST_B + tm) // SUBLANE)]
    for s in range(1, SUBLANE):
        hbd_ref[s - 1, SUBLANE:HIST_B + tm, :] = jnp.concatenate(
            _roll_down_slabs(hb_slabs, s), axis=0)

    ab = proj_ref[:, COL_AB:COL_AB + LANE]
    g_n = -jnp.exp(alog_ref[...]) * _softplus(ab + dtb_ref[...])
    beta_n = pltpu.roll(jax.nn.sigmoid(ab), LANE - DN_HEADS, axis=1)
    r_i = lax.broadcasted_iota(jnp.int32, (tm, tm), 0)
    c_i = lax.broadcasted_iota(jnp.int32, (tm, tm), 1)
    same_chunk = (r_i // CHUNK) == (c_i // CHUNK)
    bd_ones = jnp.where(same_chunk, 1.0, 0.0).astype(BF16)
    bd_tril = jnp.where(same_chunk, jnp.where(r_i >= c_i, 1.0, 0.0), 0.0).astype(BF16)
    g_hi, g_mid, g_lo = _split3(g_n)
    gc_n = _dot(bd_tril, g_hi) + _dot(bd_tril, g_mid) + _dot(bd_tril, g_lo)
    gl_n = _dot(bd_ones, g_hi) + _dot(bd_ones, g_mid) + _dot(bd_ones, g_lo)
    eg_n = jnp.exp(gc_n)
    ekt_n = jnp.exp(gl_n - gc_n)
    egl_n = jnp.exp(gl_n)
    beg_n = beta_n * eg_n
    eye = jnp.where(lax.broadcasted_iota(jnp.int32, (LANE, LANE), 0)
                    == lax.broadcasted_iota(jnp.int32, (LANE, LANE), 1), 1.0, 0.0).astype(BF16)
    c_hi, c_mid, c_lo = _split3(gc_n)
    g_t = _dot_nt(eye, c_hi) + _dot_nt(eye, c_mid) + _dot_nt(eye, c_lo)

    def lane_bcast(narrow, n, h, rows=CHUNK, width=DN_HEAD_DIM):
        r0 = n * CHUNK
        return jnp.broadcast_to(narrow[r0:r0 + rows, h:h + 1], (rows, width))

    for n in range(n_chunks):
        r0 = n * CHUNK
        conv_a = _short_conv_chunk(ha_ref, cwa_ref, r0, SHORT_CONV_W)
        y_ref[r0:r0 + CHUNK, 0:D_CONV] = proj_ref[r0:r0 + CHUNK, COL_A_B:COL_A_C] * conv_a

        acc3 = None
        for k in range(CONF_CONV_W):
            o0 = HIST_B - (CONF_CONV_W - 1) + k + r0
            lead = o0 % SUBLANE
            if lead == 0:
                win = hb_ref[o0:o0 + CHUNK, :]
            else:
                a0 = o0 - lead + SUBLANE
                win = hbd_ref[SUBLANE - lead - 1, a0:a0 + CHUNK, :]
            term = cwb_ref[k][None] * win.reshape(CHUNK // SUBLANE, SUBLANE, D_CONF)
            acc3 = term if acc3 is None else acc3 + term
        acc = acc3.reshape(CHUNK, D_CONF) + cbb_ref[...]
        mu = jnp.mean(acc, axis=-1, keepdims=True)
        dev = acc - mu
        var = jnp.mean(dev * dev, axis=-1, keepdims=True)
        yb = dev * lax.rsqrt(var + LN_EPS) * lng_ref[...] + lnb_ref[...]
        y_ref[r0:r0 + CHUNK, D_CONV:D_CONV + D_CONF] = _silu(yb)

        qkv = _silu(_short_conv_chunk(hc_ref, cwc_ref, r0, DN_CONV_W))
        for h in range(DN_HEADS):
            l0 = h * DN_HEAD_DIM
            qh = qkv[:, l0:l0 + DN_HEAD_DIM]
            kh = qkv[:, D_DN + l0:D_DN + l0 + DN_HEAD_DIM]
            qn = lax.rsqrt(jnp.sum(qh * qh, axis=-1, keepdims=True) + EPS) * Q_SCALE
            kn = lax.rsqrt(jnp.sum(kh * kh, axis=-1, keepdims=True) + EPS)
            qkv_ref[r0:r0 + CHUNK, l0:l0 + DN_HEAD_DIM] = qh * qn
            qkv_ref[r0:r0 + CHUNK, D_DN + l0:D_DN + l0 + DN_HEAD_DIM] = kh * kn
        qkv_ref[r0:r0 + CHUNK, 2 * D_DN:] = qkv[:, 2 * D_DN:]

    ha_ref[0:HIST_A, :] = ha_ref[tm:tm + HIST_A, :]
    hb_ref[0:HIST_B, :] = hb_ref[tm:tm + HIST_B, :]
    hc_ref[0:HIST_C, :] = hc_ref[tm:tm + HIST_C, :]

    def head_slice(ref, n, h, base=0):
        r0 = n * CHUNK
        l0 = base + h * DN_HEAD_DIM
        return ref[r0:r0 + CHUNK, l0:l0 + DN_HEAD_DIM]

    row = lax.broadcasted_iota(jnp.int32, (CHUNK, CHUNK), 0)
    col = lax.broadcasted_iota(jnp.int32, (CHUNK, CHUNK), 1)
    causal = row >= col
    strict = row > col

    raws = []
    for n, h in items:
        kh = head_slice(qkv_ref, n, h, D_DN)
        kb = kh * lane_bcast(beta_n, n, h)
        lhs = jnp.concatenate([kb, head_slice(qkv_ref, n, h)], axis=0).astype(BF16)
        raws.append(_dot_nt(lhs, kh.astype(BF16)))
    xs = []
    for (n, h), raw in zip(items, raws):
        r0 = n * CHUNK
        diff = lane_bcast(gc_n, n, h, width=CHUNK) - g_t[h:h + 1, r0:r0 + CHUNK]
        decay = jnp.where(causal, jnp.exp(jnp.where(causal, diff, 0.0)), 0.0)
        xs.append(jnp.where(strict, -(raw[:CHUNK] * decay), 0.0))
        qk_ref[r0:r0 + CHUNK, h * CHUNK:(h + 1) * CHUNK] = (raw[CHUNK:] * decay).astype(BF16)
    ys = xs
    ps = [_dot(x.astype(BF16), x.astype(BF16)) for x in xs]
    m = 2
    while m < CHUNK:
        last = 2 * m >= CHUNK
        if last:
            prods = [_dot(p.astype(BF16), y.astype(BF16)) for p, y in zip(ps, ys)]
            ys = [y + p + pr for y, p, pr in zip(ys, ps, prods)]
        else:
            prods = [_dot(p.astype(BF16), jnp.concatenate([y, p], axis=1).astype(BF16))
                     for p, y in zip(ps, ys)]
            ys = [y + p + pr[:, :CHUNK] for y, p, pr in zip(ys, ps, prods)]
            ps = [pr[:, CHUNK:] for pr in prods]
        m *= 2
    for (n, h), y in zip(items, ys):
        r0 = n * CHUNK
        l0 = h * DN_HEAD_DIM
        kh = head_slice(qkv_ref, n, h, D_DN)
        vb = head_slice(qkv_ref, n, h, 2 * D_DN) * lane_bcast(beta_n, n, h)
        kbg = kh * lane_bcast(beg_n, n, h)
        uw = _dot(y.astype(BF16), jnp.concatenate([vb, kbg], axis=1).astype(BF16))
        u_ref[r0:r0 + CHUNK, l0:l0 + DN_HEAD_DIM] = vb + uw[:, :DN_HEAD_DIM]
        w_ref[r0:r0 + CHUNK, l0:l0 + DN_HEAD_DIM] = (kbg + uw[:, DN_HEAD_DIM:]).astype(BF16)
        qe_ref[r0:r0 + CHUNK, l0:l0 + DN_HEAD_DIM] = (
            head_slice(qkv_ref, n, h) * lane_bcast(eg_n, n, h)).astype(BF16)
        kt_ref[r0:r0 + CHUNK, l0:l0 + DN_HEAD_DIM] = (kh * lane_bcast(ekt_n, n, h)).astype(BF16)

    heads = range(DN_HEADS)
    for n in range(n_chunks):
        r0 = n * CHUNK
        sbs = [s_ref[h].astype(BF16) for h in heads]
        wq = [_dot(jnp.concatenate([head_slice(w_ref, n, h), head_slice(qe_ref, n, h)], axis=0),
                   sbs[h]) for h in heads]
        vn = [(head_slice(u_ref, n, h) - wq[h][:CHUNK]).astype(BF16) for h in heads]
        os_ = [wq[h][CHUNK:] + _dot(qk_ref[r0:r0 + CHUNK, h * CHUNK:(h + 1) * CHUNK], vn[h])
               for h in heads]
        for h in heads:
            decay_last = lane_bcast(egl_n, n, h, rows=1)
            s_ref[h] = s_ref[h] * decay_last + _dot_tn(head_slice(kt_ref, n, h), vn[h])
        for h in heads:
            o = os_[h]
            z = head_slice(proj_ref, n, h, COL_Z)
            on = o * lax.rsqrt(jnp.mean(o * o, axis=-1, keepdims=True) + EPS) * dng_ref[...]
            c0 = D_CONV + D_CONF + h * DN_HEAD_DIM
            y_ref[r0:r0 + CHUNK, c0:c0 + DN_HEAD_DIM] = on * _silu(z)

    mix = _dot(y_ref[...].astype(BF16), wout_ref[...])
    o_ref[...] = x_ref[...] + mod_ref[2:3, :] * mix


def _full(shape):
    nd = len(shape)
    return pl.BlockSpec(shape, lambda i, j: (0,) * nd)


def _mix_call(proj, x, mod, cwa, cwb, cbb, lng, lnb, cwc, alog, dtb, dng, wout, tm):
    b, t, d = x.shape
    n = proj.shape[2]
    small = [cwa, cwb, cbb, lng, lnb, cwc, alog, dtb, dng, wout]
    return pl.pallas_call(
        _mix_kernel,
        grid=(b, t // tm),
        in_specs=[
            pl.BlockSpec((None, tm, n), lambda i, j: (i, j, 0)),
            pl.BlockSpec((None, tm, d), lambda i, j: (i, j, 0)),
            pl.BlockSpec((None, SUBLANE, d), lambda i, j: (i, 0, 0)),
        ] + [_full(a.shape) for a in small],
        out_specs=pl.BlockSpec((None, tm, d), lambda i, j: (i, j, 0)),
        out_shape=jax.ShapeDtypeStruct((b, t, d), F32),
        scratch_shapes=[
            pltpu.VMEM((HIST_A + tm, D_CONV), F32),
            pltpu.VMEM((HIST_B + tm, D_CONF), F32),
            pltpu.VMEM((HIST_C + tm, 3 * D_DN), F32),
            pltpu.VMEM((DN_HEADS, DN_HEAD_DIM, DN_HEAD_DIM), F32),
            pltpu.VMEM((tm, d), F32),
            pltpu.VMEM((tm, 3 * D_DN), F32),
            pltpu.VMEM((tm, D_DN), F32),
            pltpu.VMEM((tm, D_DN), BF16),
            pltpu.VMEM((tm, D_DN), BF16),
            pltpu.VMEM((tm, D_DN), BF16),
            pltpu.VMEM((tm, DN_HEADS * CHUNK), BF16),
            pltpu.VMEM((SUBLANE - 1, HIST_B + tm, D_CONF), F32),
        ],
        compiler_params=pltpu.CompilerParams(
            dimension_semantics=("arbitrary", "arbitrary"), vmem_limit_bytes=VMEM_LIMIT),
        name="mix",
    )(proj, x, mod, *small)


def _ffn_kernel(x_ref, mod_ref, g_ref, w1_ref, w2_ref, fg_ref, o_ref, a_ref, *, final, n_split):
    x = x_ref[...]
    h = _modulated_rmsnorm(x, g_ref[...], mod_ref[4:5, :], mod_ref[3:4, :]).astype(BF16)
    d_ff = w2_ref.shape[0]
    step = d_ff // n_split
    for j in range(n_split):
        c0 = j * step
        gate = _dot(h, w1_ref[:, c0:c0 + step])
        up = _dot(h, w1_ref[:, d_ff + c0:d_ff + c0 + step])
        a_ref[:, c0:c0 + step] = (_silu(gate) * up).astype(BF16)
    xn = x + mod_ref[5:6, :] * _dot(a_ref[...], w2_ref[...])
    if final:
        xn = xn * lax.rsqrt(jnp.mean(xn * xn, axis=-1, keepdims=True) + EPS) * fg_ref[...]
    o_ref[...] = xn


def _ffn_call(x, mod, g, w1, w2, fg, tm, final):
    b, t, d = x.shape
    d_ff = w2.shape[0]
    return pl.pallas_call(
        functools.partial(_ffn_kernel, final=final, n_split=2),
        grid=(b, t // tm),
        in_specs=[
            pl.BlockSpec((None, tm, d), lambda i, j: (i, j, 0)),
            pl.BlockSpec((None, SUBLANE, d), lambda i, j: (i, 0, 0)),
            _full(g.shape), _full(w1.shape), _full(w2.shape), _full(fg.shape),
        ],
        out_specs=pl.BlockSpec((None, tm, d), lambda i, j: (i, j, 0)),
        out_shape=jax.ShapeDtypeStruct((b, t, d), F32),
        scratch_shapes=[pltpu.VMEM((tm, d_ff), BF16)],
        compiler_params=pltpu.CompilerParams(
            dimension_semantics=("arbitrary", "arbitrary"), vmem_limit_bytes=VMEM_LIMIT),
        name="ffn",
    )(x, mod, g, w1, w2, fg)


def _lane_row(v):
    return jnp.pad(v.astype(F32), (0, LANE - v.shape[0]))[None, :]


def _tap_tiles(w):
    return jnp.broadcast_to(w[:, None, :], (w.shape[0], SUBLANE, w.shape[1]))


def kernel(x, c, w_ada, b_ada, norm_mix_g, norm_ffn_g, w_in, conv_a_w, conf_dw_w, conf_dw_b,
           conf_ln_g, conf_ln_b, dn_conv_w, dn_a_log, dn_dt_bias, dn_norm_g, w_out,
           w_ffn_in, w_ffn_out, final_norm_g):
    bsz, t, d = x.shape
    depth = w_ada.shape[0]
    assert t % CHUNK == 0 and bsz <= SUBLANE
    tm_mix = min(256, t)
    tm_mm = min(256, t)
    tm_ffn = min(512, t)

    c_pad = jnp.pad(c, ((0, SUBLANE - bsz), (0, 0)))
    mod_all = _ada_call(c_pad, w_ada, b_ada)
    mod_all = mod_all[:, :bsz].reshape(depth, bsz, N_MOD, d)
    mod_all = jnp.pad(mod_all, ((0, 0), (0, 0), (0, SUBLANE - N_MOD), (0, 0)))
    w_in_p = jnp.pad(w_in, ((0, 0), (0, 0), (0, IN_COLS_PAD - IN_COLS))).astype(BF16)
    w_out_b = w_out.astype(BF16)
    w1_b = w_ffn_in.astype(BF16)
    w2_b = w_ffn_out.astype(BF16)

    for l in range(depth):
        mod = mod_all[l]
        proj = _inproj_call(x, mod, norm_mix_g[l][None, :], w_in_p[l], tm_mm)
        x = _mix_call(proj, x, mod, _tap_tiles(conv_a_w[l]), _tap_tiles(conf_dw_w[l]),
                      conf_dw_b[l][None, :],
                      conf_ln_g[l][None, :], conf_ln_b[l][None, :], _tap_tiles(dn_conv_w[l]),
                      _lane_row(dn_a_log[l]), _lane_row(dn_dt_bias[l]), dn_norm_g[l][None, :],
                      w_out_b[l], tm_mix)
        x = _ffn_call(x, mod, norm_ffn_g[l][None, :], w1_b[l], w2_b[l], final_norm_g[None, :],
                      tm_ffn, final=(l == depth - 1))
    return x
```

```python
import functools

import jax
import jax.numpy as jnp
from jax import lax
from jax.experimental import pallas as pl
from jax.experimental.pallas import tpu as pltpu

F32 = jnp.float32
BF16 = jnp.bfloat16

D_CONV = 256
D_CONF = 256
D_DN = 512
DN_HEADS = 4
DN_HEAD_DIM = 128
SHORT_CONV_W = 3
CONF_CONV_W = 31
DN_CONV_W = 4
CHUNK = 64
N_MOD = 6
EPS = 1e-6
LN_EPS = 1e-5
Q_SCALE = DN_HEAD_DIM ** -0.5

LANE = 128
SUBLANE = 8
VMEM_LIMIT = 56 * 1024 * 1024

COL_A_B = 0
COL_A_C = COL_A_B + D_CONV
COL_A_V = COL_A_C + D_CONV
COL_B_A = COL_A_V + D_CONV
COL_B_G = COL_B_A + D_CONF
COL_QKV = COL_B_G + D_CONF
COL_Z = COL_QKV + 3 * D_DN
COL_AB = COL_Z + D_DN
IN_COLS = COL_AB + 2 * DN_HEADS
IN_COLS_PAD = COL_AB + LANE

HIST_A = SUBLANE
HIST_B = 32
HIST_C = SUBLANE


def _dot(a, b):
    return jnp.dot(a, b, preferred_element_type=F32)


def _dot_nt(a, b):
    return lax.dot_general(a, b, (((1,), (1,)), ((), ())), preferred_element_type=F32)


def _dot_tn(a, b):
    return lax.dot_general(a, b, (((0,), (0,)), ((), ())), preferred_element_type=F32)


def _split3(a):
    hi = a.astype(BF16)
    r1 = a - hi.astype(F32)
    mid = r1.astype(BF16)
    lo = (r1 - mid.astype(F32)).astype(BF16)
    return hi, mid, lo


def _silu(v):
    return v * jax.nn.sigmoid(v)


def _softplus(v):
    return jnp.maximum(v, 0.0) + jnp.log1p(jnp.exp(-jnp.abs(v)))


def _ada_kernel(c_ref, w_ref, b_ref, o_ref):
    ca = _silu(c_ref[...])
    o_ref[...] = _dot(ca.astype(BF16), w_ref[...].astype(BF16)) + b_ref[...]


def _ada_call(c_pad, w_ada, b_ada):
    depth, d, n = w_ada.shape
    rows = c_pad.shape[0]
    tn = 1536
    return pl.pallas_call(
        _ada_kernel,
        grid=(depth, n // tn),
        in_specs=[
            pl.BlockSpec((rows, d), lambda l, j: (0, 0)),
            pl.BlockSpec((None, d, tn), lambda l, j: (l, 0, j)),
            pl.BlockSpec((None, 1, tn), lambda l, j: (l, 0, j)),
        ],
        out_specs=pl.BlockSpec((None, rows, tn), lambda l, j: (l, 0, j)),
        out_shape=jax.ShapeDtypeStruct((depth, rows, n), F32),
        compiler_params=pltpu.CompilerParams(
            dimension_semantics=("arbitrary", "arbitrary"), vmem_limit_bytes=VMEM_LIMIT),
        name="adaln_mod",
    )(c_pad, w_ada, b_ada.reshape(depth, 1, n))


def _modulated_rmsnorm(x, g, scale, shift):
    ms = jnp.mean(x * x, axis=-1, keepdims=True)
    return x * lax.rsqrt(ms + EPS) * (g * (1.0 + scale)) + shift


def _roll_down_slabs(slabs, s):
    c = slabs[0].shape[1]
    take_prev = lax.broadcasted_iota(jnp.int32, (SUBLANE, c), 0) < s
    rots = [pltpu.roll(v, s, axis=0) for v in slabs]
    return [jnp.where(take_prev, rots[i - 1], rots[i]) for i in range(1, len(slabs))]


def _short_conv_chunk(hist_ref, w_ref, r0, taps, c0, c1):
    n_slab = CHUNK // SUBLANE
    slabs = [hist_ref[r0 + SUBLANE * i:r0 + SUBLANE * (i + 1), c0:c1] for i in range(n_slab + 1)]
    acc = [w_ref[taps - 1, :, c0:c1] * v for v in slabs[1:]]
    for s in range(1, taps):
        delayed = _roll_down_slabs(slabs, s)
        acc = [a + w_ref[taps - 1 - s, :, c0:c1] * v for a, v in zip(acc, delayed)]
    return jnp.concatenate(acc, axis=0)


def _mix_kernel(x_ref, mod_ref, g_ref, win_ref, cwa_ref, cwb_ref, cbb_ref, lng_ref,
                lnb_ref, cwc_ref, alog_ref, dtb_ref, dng_ref, wout_ref, o_ref,
                ha_ref, hb_ref, hc_ref, s_ref, y_ref, qkv_ref, u_ref, w_ref, qe_ref, kt_ref,
                qk_ref, hbd_ref, ab_ref, zg_ref, hin_ref):
    tm = x_ref.shape[0]
    n_chunks = tm // CHUNK
    items = [(n, h) for n in range(n_chunks) for h in range(DN_HEADS)]

    @pl.when(pl.program_id(1) == 0)
    def _():
        ha_ref[0:HIST_A, :] = jnp.zeros((HIST_A, D_CONV), F32)
        hb_ref[0:HIST_B, :] = jnp.zeros((HIST_B, D_CONF), F32)
        hc_ref[0:HIST_C, :] = jnp.zeros((HIST_C, 3 * D_DN), F32)
        s_ref[...] = jnp.zeros(s_ref.shape, F32)

    def project_delta_inputs(r0, r1):
        hin_ref[r0:r1, :] = _modulated_rmsnorm(
            x_ref[r0:r1, :], g_ref[...], mod_ref[1:2, :], mod_ref[0:1, :]).astype(BF16)
        zg_ref[r0:r1, :] = _dot(hin_ref[r0:r1, :], win_ref[:, COL_Z:IN_COLS_PAD])
        hc_ref[HIST_C + r0:HIST_C + r1, :] = _dot(hin_ref[r0:r1, :], win_ref[:, COL_QKV:COL_Z])

    def delta_front_end(n):
        r0 = n * CHUNK
        for part in range(3):
            c0 = part * D_DN
            act = _silu(_short_conv_chunk(hc_ref, cwc_ref, r0, DN_CONV_W, c0, c0 + D_DN))
            if part == 2:
                qkv_ref[r0:r0 + CHUNK, c0:c0 + D_DN] = act
                continue
            for h in range(DN_HEADS):
                l0 = h * DN_HEAD_DIM
                xh = act[:, l0:l0 + DN_HEAD_DIM]
                inv = lax.rsqrt(jnp.sum(xh * xh, axis=-1, keepdims=True) + EPS)
                if part == 0:
                    inv = inv * Q_SCALE
                qkv_ref[r0:r0 + CHUNK, c0 + l0:c0 + l0 + DN_HEAD_DIM] = xh * inv

    project_delta_inputs(0, tm)
    for n in range(n_chunks):
        delta_front_end(n)
    hc_ref[0:HIST_C, :] = hc_ref[tm:tm + HIST_C, :]

    def project_mixers_a_b():
        a_cv = _dot(hin_ref[...], win_ref[:, COL_A_C:COL_B_A])
        ha_ref[HIST_A:HIST_A + tm, :] = a_cv[:, :D_CONV] * a_cv[:, D_CONV:]
        b_ag = _dot(hin_ref[...], win_ref[:, COL_B_A:COL_QKV])
        hb_ref[HIST_B:HIST_B + tm, :] = b_ag[:, :D_CONF] * jax.nn.sigmoid(b_ag[:, D_CONF:])
        ab_ref[...] = _dot(hin_ref[...], win_ref[:, COL_A_B:COL_A_C])

    def build_delayed_copies():
        hb_slabs = [hb_ref[SUBLANE * i:SUBLANE * (i + 1), :]
                    for i in range((HIST_B + tm) // SUBLANE)]
        for s in range(1, SUBLANE):
            hbd_ref[s - 1, SUBLANE:HIST_B + tm, :] = jnp.concatenate(
                _roll_down_slabs(hb_slabs, s), axis=0)

    ab = zg_ref[:, D_DN:D_DN + LANE]
    g_n = -jnp.exp(alog_ref[...]) * _softplus(ab + dtb_ref[...])
    beta_n = pltpu.roll(jax.nn.sigmoid(ab), LANE - DN_HEADS, axis=1)
    r_i = lax.broadcasted_iota(jnp.int32, (tm, tm), 0)
    c_i = lax.broadcasted_iota(jnp.int32, (tm, tm), 1)
    same_chunk = (r_i // CHUNK) == (c_i // CHUNK)
    bd_ones = jnp.where(same_chunk, 1.0, 0.0).astype(BF16)
    bd_tril = jnp.where(same_chunk, jnp.where(r_i >= c_i, 1.0, 0.0), 0.0).astype(BF16)
    g_hi, g_mid, g_lo = _split3(g_n)
    gc_n = _dot(bd_tril, g_hi) + _dot(bd_tril, g_mid) + _dot(bd_tril, g_lo)
    gl_n = _dot(bd_ones, g_hi) + _dot(bd_ones, g_mid) + _dot(bd_ones, g_lo)
    eg_n = jnp.exp(gc_n)
    ekt_n = jnp.exp(gl_n - gc_n)
    egl_n = jnp.exp(gl_n)
    beg_n = beta_n * eg_n
    eye = jnp.where(lax.broadcasted_iota(jnp.int32, (LANE, LANE), 0)
                    == lax.broadcasted_iota(jnp.int32, (LANE, LANE), 1), 1.0, 0.0).astype(BF16)
    c_hi, c_mid, c_lo = _split3(gc_n)
    g_t = _dot_nt(eye, c_hi) + _dot_nt(eye, c_mid) + _dot_nt(eye, c_lo)

    def lane_bcast(narrow, n, h, rows=CHUNK, width=DN_HEAD_DIM):
        r0 = n * CHUNK
        return jnp.broadcast_to(narrow[r0:r0 + rows, h:h + 1], (rows, width))

    def mixers_a_b(n):
        r0 = n * CHUNK
        conv_a = _short_conv_chunk(ha_ref, cwa_ref, r0, SHORT_CONV_W, 0, D_CONV)
        y_ref[r0:r0 + CHUNK, 0:D_CONV] = ab_ref[r0:r0 + CHUNK, :] * conv_a

        acc3 = None
        for k in range(CONF_CONV_W):
            o0 = HIST_B - (CONF_CONV_W - 1) + k + r0
            lead = o0 % SUBLANE
            if lead == 0:
                win = hb_ref[o0:o0 + CHUNK, :]
            else:
                a0 = o0 - lead + SUBLANE
                win = hbd_ref[SUBLANE - lead - 1, a0:a0 + CHUNK, :]
            term = cwb_ref[k][None] * win.reshape(CHUNK // SUBLANE, SUBLANE, D_CONF)
            acc3 = term if acc3 is None else acc3 + term
        acc = acc3.reshape(CHUNK, D_CONF) + cbb_ref[...]
        mu = jnp.mean(acc, axis=-1, keepdims=True)
        dev = acc - mu
        var = jnp.mean(dev * dev, axis=-1, keepdims=True)
        yb = dev * lax.rsqrt(var + LN_EPS) * lng_ref[...] + lnb_ref[...]
        y_ref[r0:r0 + CHUNK, D_CONV:D_CONV + D_CONF] = _silu(yb)


    def head_slice(ref, n, h, base=0):
        r0 = n * CHUNK
        l0 = base + h * DN_HEAD_DIM
        return ref[r0:r0 + CHUNK, l0:l0 + DN_HEAD_DIM]

    row = lax.broadcasted_iota(jnp.int32, (CHUNK, CHUNK), 0)
    col = lax.broadcasted_iota(jnp.int32, (CHUNK, CHUNK), 1)
    causal = row >= col
    strict = row > col

    raws = []
    for n, h in items:
        kh = head_slice(qkv_ref, n, h, D_DN)
        kb = kh * lane_bcast(beta_n, n, h)
        lhs = jnp.concatenate([kb, head_slice(qkv_ref, n, h)], axis=0).astype(BF16)
        raws.append(_dot_nt(lhs, kh.astype(BF16)))
    xs = []
    for (n, h), raw in zip(items, raws):
        r0 = n * CHUNK
        diff = lane_bcast(gc_n, n, h, width=CHUNK) - g_t[h:h + 1, r0:r0 + CHUNK]
        decay = jnp.where(causal, jnp.exp(jnp.where(causal, diff, 0.0)), 0.0)
        xs.append(jnp.where(strict, -(raw[:CHUNK] * decay), 0.0))
        qk_ref[r0:r0 + CHUNK, h * CHUNK:(h + 1) * CHUNK] = (raw[CHUNK:] * decay).astype(BF16)
    project_mixers_a_b()
    ys = xs
    ps = [_dot(x.astype(BF16), x.astype(BF16)) for x in xs]
    m = 2
    while m < CHUNK:
        last = 2 * m >= CHUNK
        if last:
            prods = [_dot(p.astype(BF16), y.astype(BF16)) for p, y in zip(ps, ys)]
            ys = [y + p + pr for y, p, pr in zip(ys, ps, prods)]
        else:
            prods = [_dot(p.astype(BF16), jnp.concatenate([y, p], axis=1).astype(BF16))
                     for p, y in zip(ps, ys)]
            ys = [y + p + pr[:, :CHUNK] for y, p, pr in zip(ys, ps, prods)]
            ps = [pr[:, CHUNK:] for pr in prods]
        m *= 2
    for (n, h), y in zip(items, ys):
        r0 = n * CHUNK
        l0 = h * DN_HEAD_DIM
        kh = head_slice(qkv_ref, n, h, D_DN)
        vb = head_slice(qkv_ref, n, h, 2 * D_DN) * lane_bcast(beta_n, n, h)
        kbg = kh * lane_bcast(beg_n, n, h)
        uw = _dot(y.astype(BF16), jnp.concatenate([vb, kbg], axis=1).astype(BF16))
        u_ref[r0:r0 + CHUNK, l0:l0 + DN_HEAD_DIM] = vb + uw[:, :DN_HEAD_DIM]
        w_ref[r0:r0 + CHUNK, l0:l0 + DN_HEAD_DIM] = (kbg + uw[:, DN_HEAD_DIM:]).astype(BF16)
        qe_ref[r0:r0 + CHUNK, l0:l0 + DN_HEAD_DIM] = (
            head_slice(qkv_ref, n, h) * lane_bcast(eg_n, n, h)).astype(BF16)
        kt_ref[r0:r0 + CHUNK, l0:l0 + DN_HEAD_DIM] = (kh * lane_bcast(ekt_n, n, h)).astype(BF16)

    build_delayed_copies()
    heads = range(DN_HEADS)
    for n in range(n_chunks):
        r0 = n * CHUNK
        sbs = [s_ref[h].astype(BF16) for h in heads]
        wq = [_dot(jnp.concatenate([head_slice(w_ref, n, h), head_slice(qe_ref, n, h)], axis=0),
                   sbs[h]) for h in heads]
        vn = [(head_slice(u_ref, n, h) - wq[h][:CHUNK]).astype(BF16) for h in heads]
        os_ = [wq[h][CHUNK:] + _dot(qk_ref[r0:r0 + CHUNK, h * CHUNK:(h + 1) * CHUNK], vn[h])
               for h in heads]
        for h in heads:
            decay_last = lane_bcast(egl_n, n, h, rows=1)
            s_ref[h] = s_ref[h] * decay_last + _dot_tn(head_slice(kt_ref, n, h), vn[h])
        for h in heads:
            o = os_[h]
            z = head_slice(zg_ref, n, h)
            on = o * lax.rsqrt(jnp.mean(o * o, axis=-1, keepdims=True) + EPS) * dng_ref[...]
            c0 = D_CONV + D_CONF + h * DN_HEAD_DIM
            y_ref[r0:r0 + CHUNK, c0:c0 + DN_HEAD_DIM] = on * _silu(z)
        mixers_a_b(n)

    ha_ref[0:HIST_A, :] = ha_ref[tm:tm + HIST_A, :]
    hb_ref[0:HIST_B, :] = hb_ref[tm:tm + HIST_B, :]

    mix = _dot(y_ref[...].astype(BF16), wout_ref[...])
    o_ref[...] = x_ref[...] + mod_ref[2:3, :] * mix


def _full(shape):
    nd = len(shape)
    return pl.BlockSpec(shape, lambda i, j: (0,) * nd)


def _mix_call(x, mod, g, win, cwa, cwb, cbb, lng, lnb, cwc, alog, dtb, dng, wout, tm):
    b, t, d = x.shape
    small = [g, win, cwa, cwb, cbb, lng, lnb, cwc, alog, dtb, dng, wout]
    return pl.pallas_call(
        _mix_kernel,
        grid=(b, t // tm),
        in_specs=[
            pl.BlockSpec((None, tm, d), lambda i, j: (i, j, 0)),
            pl.BlockSpec((None, SUBLANE, d), lambda i, j: (i, 0, 0)),
        ] + [_full(a.shape) for a in small],
        out_specs=pl.BlockSpec((None, tm, d), lambda i, j: (i, j, 0)),
        out_shape=jax.ShapeDtypeStruct((b, t, d), F32),
        scratch_shapes=[
            pltpu.VMEM((HIST_A + tm, D_CONV), F32),
            pltpu.VMEM((HIST_B + tm, D_CONF), F32),
            pltpu.VMEM((HIST_C + tm, 3 * D_DN), F32),
            pltpu.VMEM((DN_HEADS, DN_HEAD_DIM, DN_HEAD_DIM), F32),
            pltpu.VMEM((tm, d), F32),
            pltpu.VMEM((tm, 3 * D_DN), F32),
            pltpu.VMEM((tm, D_DN), F32),
            pltpu.VMEM((tm, D_DN), BF16),
            pltpu.VMEM((tm, D_DN), BF16),
            pltpu.VMEM((tm, D_DN), BF16),
            pltpu.VMEM((tm, DN_HEADS * CHUNK), BF16),
            pltpu.VMEM((SUBLANE - 1, HIST_B + tm, D_CONF), F32),
            pltpu.VMEM((tm, D_CONV), F32),
            pltpu.VMEM((tm, D_DN + LANE), F32),
            pltpu.VMEM((tm, d), BF16),
        ],
        compiler_params=pltpu.CompilerParams(
            dimension_semantics=("arbitrary", "arbitrary"), vmem_limit_bytes=VMEM_LIMIT),
        name="mix",
    )(x, mod, *small)


def _ffn_kernel(x_ref, mod_ref, g_ref, w1_ref, w2_ref, fg_ref, o_ref, a_ref, *, final, n_split):
    x = x_ref[...]
    h = _modulated_rmsnorm(x, g_ref[...], mod_ref[4:5, :], mod_ref[3:4, :]).astype(BF16)
    d_ff = w2_ref.shape[0]
    step = d_ff // n_split
    for j in range(n_split):
        c0 = j * step
        gate = _dot(h, w1_ref[:, c0:c0 + step])
        up = _dot(h, w1_ref[:, d_ff + c0:d_ff + c0 + step])
        a_ref[:, c0:c0 + step] = (_silu(gate) * up).astype(BF16)
    xn = x + mod_ref[5:6, :] * _dot(a_ref[...], w2_ref[...])
    if final:
        xn = xn * lax.rsqrt(jnp.mean(xn * xn, axis=-1, keepdims=True) + EPS) * fg_ref[...]
    o_ref[...] = xn


def _ffn_call(x, mod, g, w1, w2, fg, tm, final):
    b, t, d = x.shape
    d_ff = w2.shape[0]
    return pl.pallas_call(
        functools.partial(_ffn_kernel, final=final, n_split=2),
        grid=(b, t // tm),
        in_specs=[
            pl.BlockSpec((None, tm, d), lambda i, j: (i, j, 0)),
            pl.BlockSpec((None, SUBLANE, d), lambda i, j: (i, 0, 0)),
            _full(g.shape), _full(w1.shape), _full(w2.shape), _full(fg.shape),
        ],
        out_specs=pl.BlockSpec((None, tm, d), lambda i, j: (i, j, 0)),
        out_shape=jax.ShapeDtypeStruct((b, t, d), F32),
        scratch_shapes=[pltpu.VMEM((tm, d_ff), BF16)],
        compiler_params=pltpu.CompilerParams(
            dimension_semantics=("arbitrary", "arbitrary"), vmem_limit_bytes=VMEM_LIMIT),
        name="ffn",
    )(x, mod, g, w1, w2, fg)


def _lane_row(v):
    return jnp.pad(v.astype(F32), (0, LANE - v.shape[0]))[None, :]


def _tap_tiles(w):
    return jnp.broadcast_to(w[:, None, :], (w.shape[0], SUBLANE, w.shape[1]))


def kernel(x, c, w_ada, b_ada, norm_mix_g, norm_ffn_g, w_in, conv_a_w, conf_dw_w, conf_dw_b,
           conf_ln_g, conf_ln_b, dn_conv_w, dn_a_log, dn_dt_bias, dn_norm_g, w_out,
           w_ffn_in, w_ffn_out, final_norm_g):
    bsz, t, d = x.shape
    depth = w_ada.shape[0]
    assert t % CHUNK == 0 and bsz <= SUBLANE
    tm_mix = min(256, t)
    tm_ffn = min(512, t)

    c_pad = jnp.pad(c, ((0, SUBLANE - bsz), (0, 0)))
    mod_all = _ada_call(c_pad, w_ada, b_ada)
    mod_all = mod_all[:, :bsz].reshape(depth, bsz, N_MOD, d)
    mod_all = jnp.pad(mod_all, ((0, 0), (0, 0), (0, SUBLANE - N_MOD), (0, 0)))
    w_in_p = jnp.pad(w_in, ((0, 0), (0, 0), (0, IN_COLS_PAD - IN_COLS))).astype(BF16)
    w_out_b = w_out.astype(BF16)
    w1_b = w_ffn_in.astype(BF16)
    w2_b = w_ffn_out.astype(BF16)

    for l in range(depth):
        mod = mod_all[l]
        x = _mix_call(x, mod, norm_mix_g[l][None, :], w_in_p[l],
                      _tap_tiles(conv_a_w[l]), _tap_tiles(conf_dw_w[l]),
                      conf_dw_b[l][None, :],
                      conf_ln_g[l][None, :], conf_ln_b[l][None, :], _tap_tiles(dn_conv_w[l]),
                      _lane_row(dn_a_log[l]), _lane_row(dn_dt_bias[l]), dn_norm_g[l][None, :],
                      w_out_b[l], tm_mix)
        x = _ffn_call(x, mod, norm_ffn_g[l][None, :], w1_b[l], w2_b[l], final_norm_g[None, :],
                      tm_ffn, final=(l == depth - 1))
    return x
```
